```python
import jax, jax.numpy as jnp
from jax import lax
import numpy as np

D_MODEL = 1024
BATCH = 8
SEQ = 2048
DEPTH = 4

POOL_GROUPS = 4
POOL_GROUP_DIM = 128
POOL_WIDTH = POOL_GROUPS * POOL_GROUP_DIM
POOL_WINDOWS = (2, 4, 8, 16)
LRU_HEADS = 10
LRU_HEAD_DIM = 128
LRU_WIDTH = LRU_HEADS * LRU_HEAD_DIM
CONV_WIDTH = 4
LRU_C = 8.0
D_FF = 2816
EPS = 1e-6
IN_WIDTH = POOL_WIDTH + 2 * LRU_WIDTH + 2 * D_MODEL

kernel_name = "macaron_pool_rglru_gated_hybrid"


def rmsnorm(x, g):
    xf = x.astype(jnp.float32)
    var = jnp.mean(xf * xf, axis=-1, keepdims=True)
    return (xf * lax.rsqrt(var + EPS) * g.astype(jnp.float32)).astype(x.dtype)


def swiglu_ffn(h, w_up, w_down):
    u = h @ w_up
    a, b = jnp.split(u, 2, axis=-1)
    return (jax.nn.silu(a) * b) @ w_down


def causal_pool_minus_self(u, window):
    b, s, c = u.shape
    uf = u.astype(jnp.float32)
    cs = jnp.cumsum(uf, axis=1)
    cs_pad = jnp.concatenate([jnp.zeros((b, 1, c), jnp.float32), cs], axis=1)
    prev = jnp.concatenate([jnp.zeros((b, window - 1, c), jnp.float32), cs_pad[:, : s - window + 1]], axis=1)
    count = jnp.minimum(jnp.arange(1, s + 1, dtype=jnp.float32), float(window))[None, :, None]
    return ((cs - prev) / count - uf).astype(u.dtype)


def pool_mixer(u, w_grp, b_grp, scale):
    b, s, _ = u.shape
    ug = u.reshape(b, s, POOL_GROUPS, POOL_GROUP_DIM)
    pooled = jnp.stack([causal_pool_minus_self(ug[:, :, g], POOL_WINDOWS[g]) for g in range(POOL_GROUPS)], axis=2)
    mixed = jnp.einsum('bsgc,gcd->bsgd', pooled, w_grp) + b_grp
    return mixed.reshape(b, s, POOL_WIDTH) * scale


def causal_depthwise_conv(u, w, bias):
    s = u.shape[1]
    up = jnp.pad(u, ((0, 0), (CONV_WIDTH - 1, 0), (0, 0)))
    y = sum(up[:, k:k + s] * w[k] for k in range(CONV_WIDTH))
    return y + bias


def _lru_combine(left, right):
    a_l, b_l = left
    a_r, b_r = right
    return a_l * a_r, a_r * b_l + b_r


def rg_lru(u, w_a, b_a, w_x, b_x, lam):
    b, s, _ = u.shape
    uh = u.reshape(b, s, LRU_HEADS, LRU_HEAD_DIM)
    r = jax.nn.sigmoid(jnp.einsum('bshd,hde->bshe', uh, w_a) + b_a).reshape(b, s, LRU_WIDTH)
    i = jax.nn.sigmoid(jnp.einsum('bshd,hde->bshe', uh, w_x) + b_x).reshape(b, s, LRU_WIDTH)
    log_a = -LRU_C * r.astype(jnp.float32) * jax.nn.softplus(-lam.astype(jnp.float32))
    a = jnp.exp(log_a)
    mult = jnp.sqrt(-jnp.expm1(2.0 * log_a))
    bx = mult * (i * u).astype(jnp.float32)
    _, h = lax.associative_scan(_lru_combine, (a, bx), axis=1)
    return h.astype(u.dtype)


def hybrid_mixer(h, w_in, pool_w, pool_b, pool_scale, w_pool_up, conv_w, conv_b,
                 lru_w_a, lru_b_a, lru_w_x, lru_b_x, lru_lambda, w_lru_up, w_out):
    proj = h @ w_in
    s1 = POOL_WIDTH
    s2 = s1 + LRU_WIDTH
    s3 = s2 + LRU_WIDTH
    u_pool, u_lru, u_gelu, g_logits = proj[..., :s1], proj[..., s1:s2], proj[..., s2:s3], proj[..., s3:]
    y_pool = pool_mixer(u_pool, pool_w, pool_b, pool_scale) @ w_pool_up
    v = causal_depthwise_conv(u_lru, conv_w, conv_b)
    y_lru = (rg_lru(v, lru_w_a, lru_b_a, lru_w_x, lru_b_x, lru_lambda) * jax.nn.gelu(u_gelu)) @ w_lru_up
    g = jax.nn.sigmoid(g_logits)
    g_pool, g_lru = g[..., :D_MODEL], g[..., D_MODEL:]
    return (g_pool * y_pool + g_lru * y_lru) @ w_out


def setup_inputs(seed: int = 0) -> dict:
    key = jax.random.key(seed)
    ks = jax.random.split(key, 26)
    f32 = jnp.float32

    def nrm(k, shape, fan_in):
        return jax.random.normal(k, shape, f32) * (fan_in ** -0.5)

    def gain(k, shape):
        return 1.0 + 0.02 * jax.random.normal(k, shape, f32)

    def small(k, shape):
        return 0.01 * jax.random.normal(k, shape, f32)

    L = DEPTH
    a_c = jax.random.uniform(ks[17], (L, LRU_WIDTH), f32, 0.9, 0.999)
    sig = a_c ** (1.0 / LRU_C)
    lam = jnp.log(sig) - jnp.log1p(-sig)
    return {
        "x": jax.random.normal(ks[0], (BATCH, SEQ, D_MODEL), f32),
        "norm_ffn1": gain(ks[1], (L, D_MODEL)),
        "ffn1_w_up": nrm(ks[2], (L, D_MODEL, 2 * D_FF), D_MODEL),
        "ffn1_w_down": nrm(ks[3], (L, D_FF, D_MODEL), D_FF),
        "norm_mix": gain(ks[4], (L, D_MODEL)),
        "w_in": nrm(ks[5], (L, D_MODEL, IN_WIDTH), D_MODEL),
        "pool_w": nrm(ks[6], (L, POOL_GROUPS, POOL_GROUP_DIM, POOL_GROUP_DIM), POOL_GROUP_DIM),
        "pool_b": small(ks[7], (L, POOL_GROUPS, POOL_GROUP_DIM)),
        "pool_scale": 1.0 + 0.1 * jax.random.normal(ks[8], (L, POOL_WIDTH), f32),
        "w_pool_up": nrm(ks[9], (L, POOL_WIDTH, D_MODEL), POOL_WIDTH),
        "conv_w": nrm(ks[10], (L, CONV_WIDTH, LRU_WIDTH), CONV_WIDTH),
        "conv_b": small(ks[11], (L, LRU_WIDTH)),
        "lru_w_a": nrm(ks[12], (L, LRU_HEADS, LRU_HEAD_DIM, LRU_HEAD_DIM), LRU_HEAD_DIM),
        "lru_b_a": small(ks[13], (L, LRU_HEADS, LRU_HEAD_DIM)),
        "lru_w_x": nrm(ks[14], (L, LRU_HEADS, LRU_HEAD_DIM, LRU_HEAD_DIM), LRU_HEAD_DIM),
        "lru_b_x": small(ks[15], (L, LRU_HEADS, LRU_HEAD_DIM)),
        "lru_lambda": lam,
        "w_lru_up": nrm(ks[16], (L, LRU_WIDTH, D_MODEL), LRU_WIDTH),
        "w_out": nrm(ks[18], (L, D_MODEL, D_MODEL), D_MODEL),
        "norm_ffn2": gain(ks[19], (L, D_MODEL)),
        "ffn2_w_up": nrm(ks[20], (L, D_MODEL, 2 * D_FF), D_MODEL),
        "ffn2_w_down": nrm(ks[21], (L, D_FF, D_MODEL), D_FF),
        "final_norm": gain(ks[22], (D_MODEL,)),
    }


def reference(x, norm_ffn1, ffn1_w_up, ffn1_w_down, norm_mix, w_in, pool_w, pool_b, pool_scale,
              w_pool_up, conv_w, conv_b, lru_w_a, lru_b_a, lru_w_x, lru_b_x, lru_lambda, w_lru_up,
              w_out, norm_ffn2, ffn2_w_up, ffn2_w_down, final_norm):
    for l in range(DEPTH):
        x = x + 0.5 * swiglu_ffn(rmsnorm(x, norm_ffn1[l]), ffn1_w_up[l], ffn1_w_down[l])
        x = x + hybrid_mixer(rmsnorm(x, norm_mix[l]), w_in[l], pool_w[l], pool_b[l], pool_scale[l],
                             w_pool_up[l], conv_w[l], conv_b[l], lru_w_a[l], lru_b_a[l],
                             lru_w_x[l], lru_b_x[l], lru_lambda[l], w_lru_up[l], w_out[l])
        x = x + 0.5 * swiglu_ffn(rmsnorm(x, norm_ffn2[l]), ffn2_w_up[l], ffn2_w_down[l])
    return rmsnorm(x, final_norm)
```

```python
import functools

import jax
import jax.numpy as jnp
from jax import lax
from jax.experimental import pallas as pl
from jax.experimental.pallas import tpu as pltpu

D_MODEL = 1024
SEQ = 2048
DEPTH = 4
POOL_GROUPS = 4
POOL_GROUP_DIM = 128
POOL_WIDTH = POOL_GROUPS * POOL_GROUP_DIM
POOL_WINDOWS = (2, 4, 8, 16)
LRU_HEADS = 10
LRU_HEAD_DIM = 128
LRU_WIDTH = LRU_HEADS * LRU_HEAD_DIM
CONV_WIDTH = 4
LRU_C = 8.0
D_FF = 2816
EPS = 1e-6

OFF_POOL = 0
OFF_LRU = POOL_WIDTH
OFF_GELU = OFF_LRU + LRU_WIDTH
OFF_GPOOL = OFF_GELU + LRU_WIDTH
OFF_GLRU = OFF_GPOOL + D_MODEL
IN_WIDTH = OFF_GLRU + D_MODEL

V7X_SUBLANES = 8
V7X_VMEM_LIMIT = 60000 * 1024

FFN_TILE = 1024
FFN_CHUNK = 256
MIX_TILE = 512
POOL_HIST = 16
CONV_HIST = 8

F32 = jnp.float32
BF16 = jnp.bfloat16


def _rmsnorm(x, g):
    var = jnp.mean(x * x, axis=-1, keepdims=True)
    return x * lax.rsqrt(var + EPS) * g


def _dot(a, b):
    return jnp.dot(a, b, preferred_element_type=F32)


def _ffn_kernel(x_ref, g_ref, wup_ref, wdn_ref, o_ref, hn_ref, acc_ref):
    x = x_ref[...]
    hn_ref[...] = _rmsnorm(x, g_ref[...]).astype(BF16)
    for j in range(D_FF // FFN_CHUNK):
        lo = j * FFN_CHUNK
        hn = hn_ref[...]
        a = _dot(hn, wup_ref[:, lo:lo + FFN_CHUNK])
        b = _dot(hn, wup_ref[:, D_FF + lo:D_FF + lo + FFN_CHUNK])
        gated = (a * jax.nn.sigmoid(a) * b).astype(BF16)
        contrib = _dot(gated, wdn_ref[lo:lo + FFN_CHUNK, :])
        if j == 0:
            acc_ref[...] = contrib
        else:
            acc_ref[...] += contrib
    o_ref[...] = x_ref[...] + 0.5 * acc_ref[...]


def _resident(shape, layer):
    zeros = (0,) * (len(shape) - 1)
    return pl.BlockSpec((None,) + tuple(shape[1:]), lambda i: (layer,) + zeros,
                        pipeline_mode=pl.Buffered(1))


def _ffn(x, g, w_up, w_dn, layer):
    n_tok = x.shape[0]
    return pl.pallas_call(
        _ffn_kernel,
        grid=(n_tok // FFN_TILE,),
        in_specs=[
            pl.BlockSpec((FFN_TILE, D_MODEL), lambda i: (i, 0)),
            _resident(g.shape, layer),
            _resident(w_up.shape, layer),
            _resident(w_dn.shape, layer),
        ],
        out_specs=pl.BlockSpec((FFN_TILE, D_MODEL), lambda i: (i, 0)),
        out_shape=jax.ShapeDtypeStruct(x.shape, x.dtype),
        scratch_shapes=[
            pltpu.VMEM((FFN_TILE, D_MODEL), BF16),
            pltpu.VMEM((FFN_TILE, D_MODEL), F32),
        ],
        compiler_params=pltpu.CompilerParams(
            dimension_semantics=("arbitrary",), vmem_limit_bytes=V7X_VMEM_LIMIT),
        name="ffn",
    )(x, g, w_up, w_dn)


def _shift_rows(x, d):
    return x if d == 0 else pltpu.roll(x, d, axis=0)


def _block_scan(a, b, row):
    d = 1
    while d < V7X_SUBLANES:
        keep = row >= d
        a_prev = jnp.where(keep, pltpu.roll(a, d, axis=0), 1.0)
        b_prev = jnp.where(keep, pltpu.roll(b, d, axis=0), 0.0)
        b = a * b_prev + b
        a = a * a_prev
        d *= 2
    return a, b


def _mixer_kernel(x_ref, g_ref, win_ref, poolw_ref, poolb_ref, pools_ref, wpu_ref,
                  convw_ref, convb_ref, wa_ref, ba_ref, wx_ref, bx_ref, lam_ref,
                  wlu_ref, wout_ref, o_ref,
                  pool_hist, conv_hist, h_carry, a_buf, b_buf, h_buf):
    tiles_per_seq = SEQ // MIX_TILE
    tile_in_seq = pl.program_id(0) % tiles_per_seq

    @pl.when(tile_in_seq == 0)
    def _():
        pool_hist[...] = jnp.zeros_like(pool_hist)
        conv_hist[...] = jnp.zeros_like(conv_hist)
        h_carry[...] = jnp.zeros_like(h_carry)

    x = x_ref[...]
    hn = _rmsnorm(x, g_ref[...]).astype(BF16)

    pos = tile_in_seq * MIX_TILE + lax.broadcasted_iota(jnp.int32, (MIX_TILE, 1), 0)

    u_pool = _dot(hn, win_ref[:, OFF_POOL:OFF_POOL + POOL_WIDTH])
    ext = jnp.concatenate([pool_hist[...], u_pool], axis=0)
    pool_hist[...] = u_pool[MIX_TILE - POOL_HIST:, :]
    mixed = []
    for grp in range(POOL_GROUPS):
        cols = slice(grp * POOL_GROUP_DIM, (grp + 1) * POOL_GROUP_DIM)
        window = POOL_WINDOWS[grp]
        s = ext[:, cols]
        d = 1
        while d < window:
            s = s + _shift_rows(s, d)
            d *= 2
        count = jnp.minimum(pos + 1, window).astype(F32)
        pooled = s[POOL_HIST:, :] / count - u_pool[:, cols]
        m = _dot(pooled.astype(BF16), poolw_ref[grp]) + poolb_ref[:, cols]
        mixed.append(m * pools_ref[:, cols])
    mixed = jnp.concatenate(mixed, axis=1).astype(BF16)
    y_pool = _dot(mixed, wpu_ref[...])
    g_pool = jax.nn.sigmoid(_dot(hn, win_ref[:, OFF_GPOOL:OFF_GPOOL + D_MODEL]))
    z = g_pool * y_pool

    u_lru = _dot(hn, win_ref[:, OFF_LRU:OFF_LRU + LRU_WIDTH])
    ext = jnp.concatenate([conv_hist[...], u_lru], axis=0)
    conv_hist[...] = u_lru[MIX_TILE - CONV_HIST:, :]
    v = convb_ref[...] + convw_ref[CONV_WIDTH - 1:CONV_WIDTH, :] * u_lru
    for k in range(CONV_WIDTH - 1):
        shifted = _shift_rows(ext, CONV_WIDTH - 1 - k)[CONV_HIST:, :]
        v = v + convw_ref[k:k + 1, :] * shifted

    r_parts, i_parts = [], []
    for hd in range(LRU_HEADS):
        cols = slice(hd * LRU_HEAD_DIM, (hd + 1) * LRU_HEAD_DIM)
        vh = v[:, cols].astype(BF16)
        r_parts.append(jax.nn.sigmoid(_dot(vh, wa_ref[hd]) + ba_ref[:, cols]))
        i_parts.append(jax.nn.sigmoid(_dot(vh, wx_ref[hd]) + bx_ref[:, cols]))
    r = jnp.concatenate(r_parts, axis=1)
    gate_i = jnp.concatenate(i_parts, axis=1)

    neg_lam = -lam_ref[...]
    softplus = jnp.maximum(neg_lam, 0.0) + jnp.log1p(jnp.exp(-jnp.abs(neg_lam)))
    log_a = (-LRU_C) * r * softplus
    a = jnp.exp(log_a)
    mult = jnp.sqrt(-jnp.tanh(log_a) * (1.0 + a * a))
    a_buf[...] = a
    b_buf[...] = mult * (gate_i * v)

    row = lax.broadcasted_iota(jnp.int32, (V7X_SUBLANES, LRU_WIDTH), 0)

    def scan_step(blk, carry):
        rows = pl.ds(pl.multiple_of(blk * V7X_SUBLANES, V7X_SUBLANES), V7X_SUBLANES)
        a_blk, b_blk = _block_scan(a_buf[rows, :], b_buf[rows, :], row)
        h = b_blk + a_blk * carry
        h_buf[rows, :] = h
        return jnp.broadcast_to(h[V7X_SUBLANES - 1:, :], (V7X_SUBLANES, LRU_WIDTH))

    h_carry[...] = lax.fori_loop(0, MIX_TILE // V7X_SUBLANES, scan_step, h_carry[...])

    u_gelu = _dot(hn, win_ref[:, OFF_GELU:OFF_GELU + LRU_WIDTH])
    y_in = (h_buf[...] * jax.nn.gelu(u_gelu)).astype(BF16)
    y_lru = _dot(y_in, wlu_ref[...])
    g_lru = jax.nn.sigmoid(_dot(hn, win_ref[:, OFF_GLRU:OFF_GLRU + D_MODEL]))
    z = z + g_lru * y_lru

    o_ref[...] = x_ref[...] + _dot(z.astype(BF16), wout_ref[...])


def _mixer(x, layer, g, w_in, pool_w, pool_b, pool_scale, w_pool_up, conv_w, conv_b,
           w_a, b_a, w_x, b_x, lam, w_lru_up, w_out):
    n_tok = x.shape[0]
    params = (g, w_in, pool_w, pool_b, pool_scale, w_pool_up, conv_w, conv_b,
              w_a, b_a, w_x, b_x, lam, w_lru_up, w_out)
    return pl.pallas_call(
        _mixer_kernel,
        grid=(n_tok // MIX_TILE,),
        in_specs=[pl.BlockSpec((MIX_TILE, D_MODEL), lambda i: (i, 0))]
        + [_resident(p.shape, layer) for p in params],
        out_specs=pl.BlockSpec((MIX_TILE, D_MODEL), lambda i: (i, 0)),
        out_shape=jax.ShapeDtypeStruct(x.shape, x.dtype),
        scratch_shapes=[
            pltpu.VMEM((POOL_HIST, POOL_WIDTH), F32),
            pltpu.VMEM((CONV_HIST, LRU_WIDTH), F32),
            pltpu.VMEM((V7X_SUBLANES, LRU_WIDTH), F32),
            pltpu.VMEM((MIX_TILE, LRU_WIDTH), F32),
            pltpu.VMEM((MIX_TILE, LRU_WIDTH), F32),
            pltpu.VMEM((MIX_TILE, LRU_WIDTH), F32),
        ],
        compiler_params=pltpu.CompilerParams(
            dimension_semantics=("arbitrary",), vmem_limit_bytes=V7X_VMEM_LIMIT),
        name="mixer",
    )(x, *params)


def _final_norm_kernel(x_ref, g_ref, o_ref):
    o_ref[...] = _rmsnorm(x_ref[...], g_ref[...])


def _final_norm(x, g):
    n_tok = x.shape[0]
    return pl.pallas_call(
        _final_norm_kernel,
        grid=(n_tok // FFN_TILE,),
        in_specs=[
            pl.BlockSpec((FFN_TILE, D_MODEL), lambda i: (i, 0)),
            pl.BlockSpec((1, D_MODEL), lambda i: (0, 0)),
        ],
        out_specs=pl.BlockSpec((FFN_TILE, D_MODEL), lambda i: (i, 0)),
        out_shape=jax.ShapeDtypeStruct(x.shape, x.dtype),
        compiler_params=pltpu.CompilerParams(dimension_semantics=("arbitrary",)),
        name="final_norm",
    )(x, g)


def kernel(x, norm_ffn1, ffn1_w_up, ffn1_w_down, norm_mix, w_in, pool_w, pool_b, pool_scale,
           w_pool_up, conv_w, conv_b, lru_w_a, lru_b_a, lru_w_x, lru_b_x, lru_lambda, w_lru_up,
           w_out, norm_ffn2, ffn2_w_up, ffn2_w_down, final_norm):
    batch, seq, d_model = x.shape
    assert (seq, d_model) == (SEQ, D_MODEL) and norm_ffn1.shape[0] == DEPTH
    h = x.reshape(batch * seq, d_model)

    def row(p):
        return p.reshape(p.shape[0], 1, -1)

    bf = lambda p: p.astype(BF16)
    ffn1 = (row(norm_ffn1), bf(ffn1_w_up), bf(ffn1_w_down))
    ffn2 = (row(norm_ffn2), bf(ffn2_w_up), bf(ffn2_w_down))
    mix = (row(norm_mix), bf(w_in), bf(pool_w), row(pool_b), row(pool_scale), bf(w_pool_up),
           conv_w, row(conv_b), bf(lru_w_a), row(lru_b_a), bf(lru_w_x), row(lru_b_x),
           row(lru_lambda), bf(w_lru_up), bf(w_out))

    for layer in range(DEPTH):
        h = _ffn(h, *ffn1, layer)
        h = _mixer(h, layer, *mix)
        h = _ffn(h, *ffn2, layer)
    h = _final_norm(h, final_norm.reshape(1, d_model))
    return h.reshape(batch, seq, d_model)
```

```python
import functools

import jax
import jax.numpy as jnp
from jax import lax
from jax.experimental import pallas as pl
from jax.experimental.pallas import tpu as pltpu

D_MODEL = 1024
BATCH = 8
SEQ = 2048
DEPTH = 4
POOL_GROUPS = 4
POOL_GROUP_DIM = 128
POOL_WIDTH = POOL_GROUPS * POOL_GROUP_DIM
POOL_WINDOWS = (2, 4, 8, 16)
LRU_HEADS = 10
LRU_HEAD_DIM = 128
LRU_WIDTH = LRU_HEADS * LRU_HEAD_DIM
CONV_WIDTH = 4
LRU_C = 8.0
D_FF = 2816
EPS = 1e-6

OFF_POOL = 0
OFF_LRU = POOL_WIDTH
OFF_GELU = OFF_LRU + LRU_WIDTH
OFF_GPOOL = OFF_GELU + LRU_WIDTH
OFF_GLRU = OFF_GPOOL + D_MODEL

V7X_SUBLANES = 8
V7X_VMEM_LIMIT = 60000 * 1024

FFN_TILE = 1024
FFN_CHUNK = 256
MIX_STEPS = 128
MIX_TILE = MIX_STEPS * BATCH
POOL_HIST = max(POOL_WINDOWS) - 1
CONV_HIST = CONV_WIDTH - 1
LRU_GROUP = 2 * LRU_HEAD_DIM

F32 = jnp.float32
BF16 = jnp.bfloat16

assert BATCH == V7X_SUBLANES


def _rmsnorm(x, g):
    var = jnp.mean(x * x, axis=-1, keepdims=True)
    return x * lax.rsqrt(var + EPS) * g


def _sigmoid(x):
    return 0.5 * jnp.tanh(0.5 * x) + 0.5


_dot = functools.partial(jnp.dot, preferred_element_type=F32)


def _resident(shape, layer):
    zeros = (0,) * (len(shape) - 1)
    return pl.BlockSpec((None,) + tuple(shape[1:]), lambda i: (layer,) + zeros,
                        pipeline_mode=pl.Buffered(1))


def _ffn_kernel(x_ref, g_ref, wup_ref, wdn_ref, gfin_ref, o_ref, hn_ref, acc_ref, *, final_norm):
    x = x_ref[...]
    hn_ref[...] = _rmsnorm(x, g_ref[...]).astype(BF16)
    for j in range(D_FF // FFN_CHUNK):
        lo = j * FFN_CHUNK
        hn = hn_ref[...]
        a = _dot(hn, wup_ref[:, lo:lo + FFN_CHUNK])
        b = _dot(hn, wup_ref[:, D_FF + lo:D_FF + lo + FFN_CHUNK])
        gated = (a * _sigmoid(a) * b).astype(BF16)
        contrib = _dot(gated, wdn_ref[lo:lo + FFN_CHUNK, :])
        if j == 0:
            acc_ref[...] = contrib
        else:
            acc_ref[...] += contrib
    y = x_ref[...] + 0.5 * acc_ref[...]
    o_ref[...] = _rmsnorm(y, gfin_ref[...]) if final_norm else y


def _ffn(x, g, w_up, w_dn, g_final, layer, final_norm):
    n_rows = x.shape[0]
    return pl.pallas_call(
        functools.partial(_ffn_kernel, final_norm=final_norm),
        grid=(n_rows // FFN_TILE,),
        in_specs=[
            pl.BlockSpec((FFN_TILE, D_MODEL), lambda i: (i, 0)),
            _resident(g.shape, layer),
            _resident(w_up.shape, layer),
            _resident(w_dn.shape, layer),
            pl.BlockSpec((1, D_MODEL), lambda i: (0, 0)),
        ],
        out_specs=pl.BlockSpec((FFN_TILE, D_MODEL), lambda i: (i, 0)),
        out_shape=jax.ShapeDtypeStruct(x.shape, x.dtype),
        scratch_shapes=[
            pltpu.VMEM((FFN_TILE, D_MODEL), BF16),
            pltpu.VMEM((FFN_TILE, D_MODEL), F32),
        ],
        compiler_params=pltpu.CompilerParams(
            dimension_semantics=("arbitrary",), vmem_limit_bytes=V7X_VMEM_LIMIT),
        name="ffn_final" if final_norm else "ffn",
    )(x, g, w_up, w_dn, g_final)


def _steps(x, first, count):
    return x[first * BATCH:(first + count) * BATCH, :]


def _safe_sqrt(x):
    return x * lax.rsqrt(jnp.maximum(x, jnp.finfo(jnp.float32).tiny))


class _LruGroup:
    def __init__(self, grp, hn, win_ref, convw_ref, convb_ref, wax_ref, ba_ref, bx_ref, lam_ref,
                 conv_hist, h_carry, h_buf, yin_buf):
        self.lo = grp * LRU_GROUP
        self.cols = slice(self.lo, self.lo + LRU_GROUP)
        self.hn = hn
        self.win_ref, self.convw_ref, self.convb_ref = win_ref, convw_ref, convb_ref
        self.wax_ref, self.ba_ref, self.bx_ref, self.lam_ref = wax_ref, ba_ref, bx_ref, lam_ref
        self.conv_hist, self.h_carry, self.h_buf, self.yin_buf = conv_hist, h_carry, h_buf, yin_buf

    def project(self):
        lo, cols = self.lo, self.cols
        u_lru = _dot(self.hn, self.win_ref[:, OFF_LRU + lo:OFF_LRU + lo + LRU_GROUP])
        ext = jnp.concatenate([self.conv_hist[:, cols], u_lru], axis=0)
        self.conv_hist[:, cols] = _steps(u_lru, MIX_STEPS - CONV_HIST, CONV_HIST)
        v = self.convb_ref[:, cols] + self.convw_ref[0:1, cols] * _steps(ext, 0, MIX_STEPS)
        for k in range(1, CONV_WIDTH):
            v = v + self.convw_ref[k:k + 1, cols] * _steps(ext, k, MIX_STEPS)
        self.v = v

    def recur(self):
        lo, cols, v = self.lo, self.cols, self.v
        r_parts, i_parts = [], []
        for hd in range(lo // LRU_HEAD_DIM, (lo + LRU_GROUP) // LRU_HEAD_DIM):
            hcols = slice(hd * LRU_HEAD_DIM, (hd + 1) * LRU_HEAD_DIM)
            vh = v[:, hd * LRU_HEAD_DIM - lo:(hd + 1) * LRU_HEAD_DIM - lo].astype(BF16)
            ri = _dot(vh, self.wax_ref[hd])
            r_parts.append(_sigmoid(ri[:, :LRU_HEAD_DIM] + self.ba_ref[:, hcols]))
            i_parts.append(_sigmoid(ri[:, LRU_HEAD_DIM:] + self.bx_ref[:, hcols]))
        r = jnp.concatenate(r_parts, axis=1)
        gate_i = jnp.concatenate(i_parts, axis=1)

        neg_lam = -self.lam_ref[:, cols]
        softplus = jnp.maximum(neg_lam, 0.0) + jnp.log1p(jnp.exp(-jnp.abs(neg_lam)))
        log_a = r * ((-LRU_C) * softplus)
        a = jnp.exp(log_a)
        b = _safe_sqrt(jnp.tanh(log_a) * (-1.0 - a * a)) * (gate_i * v)

        h = self.h_carry[:, cols]
        for t in range(MIX_STEPS):
            h = _steps(a, t, 1) * h + _steps(b, t, 1)
            self.h_buf[t * BATCH:(t + 1) * BATCH, cols] = h
        self.h_carry[:, cols] = h

    def gate(self):
        lo, cols = self.lo, self.cols
        u_gelu = _dot(self.hn, self.win_ref[:, OFF_GELU + lo:OFF_GELU + lo + LRU_GROUP])
        self.yin_buf[:, cols] = (self.h_buf[:, cols] * jax.nn.gelu(u_gelu)).astype(BF16)


class _PoolBranch:
    def __init__(self, hn, tile, win_ref, poolw_ref, poolb_ref, pools_ref, wpu_ref, pool_hist):
        self.hn, self.tile, self.win_ref = hn, tile, win_ref
        self.poolw_ref, self.poolb_ref, self.pools_ref = poolw_ref, poolb_ref, pools_ref
        self.wpu_ref, self.pool_hist = wpu_ref, pool_hist

    def project(self):
        u_pool = _dot(self.hn, self.win_ref[:, OFF_POOL:OFF_POOL + POOL_WIDTH])
        ext = jnp.concatenate([self.pool_hist[...], u_pool], axis=0)
        self.pool_hist[...] = _steps(u_pool, MIX_STEPS - POOL_HIST, POOL_HIST)
        step = self.tile * MIX_STEPS + lax.shift_right_logical(
            lax.broadcasted_iota(jnp.int32, (MIX_TILE, POOL_GROUP_DIM), 0),
            BATCH.bit_length() - 1)
        self.pooled = []
        for grp in range(POOL_GROUPS):
            cols = slice(grp * POOL_GROUP_DIM, (grp + 1) * POOL_GROUP_DIM)
            window = POOL_WINDOWS[grp]
            s = _steps(ext[:, cols], POOL_HIST - (window - 1), MIX_STEPS + window - 1)
            d = 1
            while d < window:
                n = s.shape[0] // BATCH
                s = _steps(s, d, n - d) + _steps(s, 0, n - d)
                d *= 2
            count = jnp.minimum(step + 1, window).astype(F32)
            self.pooled.append((s / count - u_pool[:, cols]).astype(BF16))

    def mix(self):
        mixed = []
        for grp in range(POOL_GROUPS):
            cols = slice(grp * POOL_GROUP_DIM, (grp + 1) * POOL_GROUP_DIM)
            m = _dot(self.pooled[grp], self.poolw_ref[grp]) + self.poolb_ref[:, cols]
            mixed.append(m * self.pools_ref[:, cols])
        mixed = jnp.concatenate(mixed, axis=1).astype(BF16)
        return _dot(mixed, self.wpu_ref[...])


def _mixer_kernel(x_ref, g_ref, win_ref, poolw_ref, poolb_ref, pools_ref, wpu_ref,
                  convw_ref, convb_ref, wax_ref, ba_ref, bx_ref, lam_ref,
                  wlu_ref, wout_ref, o_ref,
                  pool_hist, conv_hist, h_carry, h_buf, yin_buf):
    tile = pl.program_id(0)

    @pl.when(tile == 0)
    def _():
        pool_hist[...] = jnp.zeros_like(pool_hist)
        conv_hist[...] = jnp.zeros_like(conv_hist)
        h_carry[...] = jnp.zeros_like(h_carry)

    hn = _rmsnorm(x_ref[...], g_ref[...]).astype(BF16)
    lru = [_LruGroup(grp, hn, win_ref, convw_ref, convb_ref, wax_ref, ba_ref, bx_ref, lam_ref,
                     conv_hist, h_carry, h_buf, yin_buf)
           for grp in range(LRU_WIDTH // LRU_GROUP)]
    pool = _PoolBranch(hn, tile, win_ref, poolw_ref, poolb_ref, pools_ref, wpu_ref, pool_hist)

    lru[0].project()
    lru[1].project()
    lru[0].recur()
    lru[0].gate()
    lru[2].project()
    lru[1].recur()
    lru[1].gate()
    lru[3].project()
    pool.project()
    lru[2].recur()
    lru[2].gate()
    lru[4].project()
    y_pool = pool.mix()
    lru[3].recur()
    lru[3].gate()
    z = _sigmoid(_dot(hn, win_ref[:, OFF_GPOOL:OFF_GPOOL + D_MODEL])) * y_pool
    lru[4].recur()
    lru[4].gate()
    g_lru = _sigmoid(_dot(hn, win_ref[:, OFF_GLRU:OFF_GLRU + D_MODEL]))
    z = z + g_lru * _dot(yin_buf[...], wlu_ref[...])

    o_ref[...] = x_ref[...] + _dot(z.astype(BF16), wout_ref[...])


def _mixer(x, layer, g, w_in, pool_w, pool_b, pool_scale, w_pool_up, conv_w, conv_b,
           w_ax, b_a, b_x, lam, w_lru_up, w_out):
    n_rows = x.shape[0]
    params = (g, w_in, pool_w, pool_b, pool_scale, w_pool_up, conv_w, conv_b,
              w_ax, b_a, b_x, lam, w_lru_up, w_out)
    return pl.pallas_call(
        _mixer_kernel,
        grid=(n_rows // MIX_TILE,),
        in_specs=[pl.BlockSpec((MIX_TILE, D_MODEL), lambda i: (i, 0))]
        + [_resident(p.shape, layer) for p in params],
        out_specs=pl.BlockSpec((MIX_TILE, D_MODEL), lambda i: (i, 0)),
        out_shape=jax.ShapeDtypeStruct(x.shape, x.dtype),
        scratch_shapes=[
            pltpu.VMEM((POOL_HIST * BATCH, POOL_WIDTH), F32),
            pltpu.VMEM((CONV_HIST * BATCH, LRU_WIDTH), F32),
            pltpu.VMEM((BATCH, LRU_WIDTH), F32),
            pltpu.VMEM((MIX_TILE, LRU_WIDTH), F32),
            pltpu.VMEM((MIX_TILE, LRU_WIDTH), BF16),
        ],
        compiler_params=pltpu.CompilerParams(
            dimension_semantics=("arbitrary",), vmem_limit_bytes=V7X_VMEM_LIMIT),
        name="mixer",
    )(x, *params)


def kernel(x, norm_ffn1, ffn1_w_up, ffn1_w_down, norm_mix, w_in, pool_w, pool_b, pool_scale,
           w_pool_up, conv_w, conv_b, lru_w_a, lru_b_a, lru_w_x, lru_b_x, lru_lambda, w_lru_up,
           w_out, norm_ffn2, ffn2_w_up, ffn2_w_down, final_norm):
    assert x.shape == (BATCH, SEQ, D_MODEL) and norm_ffn1.shape[0] == DEPTH
    h = x.transpose(1, 0, 2).reshape(SEQ * BATCH, D_MODEL)

    def row(p):
        return p.reshape(p.shape[0], 1, -1)

    bf = lambda p: p.astype(BF16)
    g_final = final_norm.reshape(1, D_MODEL)
    ffn1 = (row(norm_ffn1), bf(ffn1_w_up), bf(ffn1_w_down), g_final)
    ffn2 = (row(norm_ffn2), bf(ffn2_w_up), bf(ffn2_w_down), g_final)
    mix = (row(norm_mix), bf(w_in), bf(pool_w), row(pool_b), row(pool_scale), bf(w_pool_up),
           conv_w, row(conv_b), bf(jnp.concatenate([lru_w_a, lru_w_x], axis=-1)),
           row(lru_b_a), row(lru_b_x), row(lru_lambda), bf(w_lru_up), bf(w_out))

    for layer in range(DEPTH):
        h = _ffn(h, *ffn1, layer, False)
        h = _mixer(h, layer, *mix)
        h = _ffn(h, *ffn2, layer, layer == DEPTH - 1)
    return h.reshape(SEQ, BATCH, D_MODEL).transpose(1, 0, 2)
```

```python
import functools

import jax
import jax.numpy as jnp
from jax import lax
from jax.experimental import pallas as pl
from jax.experimental.pallas import tpu as pltpu

D_MODEL = 1024
BATCH = 8
SEQ = 2048
DEPTH = 4
POOL_GROUPS = 4
POOL_GROUP_DIM = 128
POOL_WIDTH = POOL_GROUPS * POOL_GROUP_DIM
POOL_WINDOWS = (2, 4, 8, 16)
LRU_HEADS = 10
LRU_HEAD_DIM = 128
LRU_WIDTH = LRU_HEADS * LRU_HEAD_DIM
CONV_WIDTH = 4
LRU_C = 8.0
D_FF = 2816
EPS = 1e-6

OFF_POOL = 0
OFF_LRU = POOL_WIDTH
OFF_GELU = OFF_LRU + LRU_WIDTH
OFF_GPOOL = OFF_GELU + LRU_WIDTH
OFF_GLRU = OFF_GPOOL + D_MODEL

V7X_SUBLANES = 8
V7X_VMEM_LIMIT = 60000 * 1024

FFN_TILE = 1024
FFN_CHUNK = 256
MIX_STEPS = 128
MIX_TILE = MIX_STEPS * BATCH
POOL_HIST = max(POOL_WINDOWS) - 1
CONV_HIST = CONV_WIDTH - 1
LRU_GROUP = 2 * LRU_HEAD_DIM

F32 = jnp.float32
BF16 = jnp.bfloat16

assert BATCH == V7X_SUBLANES


def _rmsnorm(x, g):
    var = jnp.mean(x * x, axis=-1, keepdims=True)
    return x * lax.rsqrt(var + EPS) * g


def _sigmoid(x):
    return 0.5 * jnp.tanh(0.5 * x) + 0.5


_dot = functools.partial(jnp.dot, preferred_element_type=F32)


def _resident(arr, layer=None):
    if layer is None:
        return pl.BlockSpec(arr.shape, lambda i: (0,) * arr.ndim, pipeline_mode=pl.Buffered(1))
    zeros = (0,) * (arr.ndim - 1)
    return pl.BlockSpec((None,) + tuple(arr.shape[1:]), lambda i: (layer,) + zeros,
                        pipeline_mode=pl.Buffered(1))


def _cast_specs(stacked, layer, n_steps):
    in_specs, out_specs, out_shapes = [], [], []
    for w in stacked:
        _, rows, cols = w.shape
        assert rows % (n_steps * 16) == 0, (w.shape, n_steps)
        blk = rows // n_steps
        in_specs.append(pl.BlockSpec((None, blk, cols), lambda i, layer=layer: (layer, i, 0)))
        out_specs.append(pl.BlockSpec((blk, cols), lambda i: (i, 0)))
        out_shapes.append(jax.ShapeDtypeStruct((rows, cols), BF16))
    return in_specs, out_specs, out_shapes


def _cast_blocks(src_refs, dst_refs):
    for s_ref, d_ref in zip(src_refs, dst_refs):
        d_ref[...] = s_ref[...].astype(BF16)


def _ffn_kernel(*refs, n_cast, final_norm, batch_major_in, batch_major_out):
    x_ref, g_ref, wup_ref, wdn_ref, gfin_ref = refs[:5]
    cast_src = refs[5:5 + n_cast]
    o_ref = refs[5 + n_cast]
    cast_dst = refs[6 + n_cast:6 + 2 * n_cast]
    hn_ref, acc_ref, x_tm = refs[6 + 2 * n_cast:]

    _cast_blocks(cast_src, cast_dst)
    if batch_major_in:
        x_tm[...] = jnp.transpose(x_ref[...], (1, 0, 2)).reshape(FFN_TILE, D_MODEL)
        x_ref = x_tm
    hn_ref[...] = _rmsnorm(x_ref[...], g_ref[...]).astype(BF16)
    for j in range(D_FF // FFN_CHUNK):
        lo = j * FFN_CHUNK
        hn = hn_ref[...]
        a = _dot(hn, wup_ref[:, lo:lo + FFN_CHUNK])
        b = _dot(hn, wup_ref[:, D_FF + lo:D_FF + lo + FFN_CHUNK])
        gated = (a * _sigmoid(a) * b).astype(BF16)
        contrib = _dot(gated, wdn_ref[lo:lo + FFN_CHUNK, :])
        if j == 0:
            acc_ref[...] = contrib
        else:
            acc_ref[...] += contrib
    y = x_ref[...] + 0.5 * acc_ref[...]
    if final_norm:
        y = _rmsnorm(y, gfin_ref[...])
    if batch_major_out:
        y = jnp.transpose(y.reshape(FFN_TILE // BATCH, BATCH, D_MODEL), (1, 0, 2))
    o_ref[...] = y


def _ffn(x, g, layer, w_up, w_dn, g_final, cast_next, cast_layer, *, final_norm=False,
         batch_major_in=False, batch_major_out=False):
    n_steps = SEQ * BATCH // FFN_TILE
    steps_per_tile = FFN_TILE // BATCH
    bm_spec = pl.BlockSpec((BATCH, steps_per_tile, D_MODEL), lambda i: (0, i, 0))
    tm_spec = pl.BlockSpec((FFN_TILE, D_MODEL), lambda i: (i, 0))
    cast_in, cast_out, cast_shapes = _cast_specs(cast_next, cast_layer, n_steps)
    out_shape = (BATCH, SEQ, D_MODEL) if batch_major_out else (SEQ * BATCH, D_MODEL)
    outs = pl.pallas_call(
        functools.partial(_ffn_kernel, n_cast=len(cast_next), final_norm=final_norm,
                          batch_major_in=batch_major_in, batch_major_out=batch_major_out),
        grid=(n_steps,),
        in_specs=[
            bm_spec if batch_major_in else tm_spec,
            _resident(g, layer),
            _resident(w_up),
            _resident(w_dn),
            _resident(g_final),
        ] + cast_in,
        out_specs=[bm_spec if batch_major_out else tm_spec] + cast_out,
        out_shape=[jax.ShapeDtypeStruct(out_shape, F32)] + cast_shapes,
        scratch_shapes=[
            pltpu.VMEM((FFN_TILE, D_MODEL), BF16),
            pltpu.VMEM((FFN_TILE, D_MODEL), F32),
            pltpu.VMEM((FFN_TILE, D_MODEL), F32),
        ],
        compiler_params=pltpu.CompilerParams(
            dimension_semantics=("arbitrary",), vmem_limit_bytes=V7X_VMEM_LIMIT),
        name="ffn",
    )(x, g, w_up, w_dn, g_final, *cast_next)
    return outs[0], tuple(outs[1:])


def _steps(x, first, count):
    return x[first * BATCH:(first + count) * BATCH, :]


def _safe_sqrt(x):
    return x * lax.rsqrt(jnp.maximum(x, jnp.finfo(jnp.float32).tiny))


class _LruGroup:
    def __init__(self, grp, hn, win_ref, convw_ref, convb_ref, wax_ref, ba_ref, bx_ref, lam_ref,
                 conv_hist, h_carry, h_buf, yin_buf):
        self.lo = grp * LRU_GROUP
        self.cols = slice(self.lo, self.lo + LRU_GROUP)
        self.hn = hn
        self.win_ref, self.convw_ref, self.convb_ref = win_ref, convw_ref, convb_ref
        self.wax_ref, self.ba_ref, self.bx_ref, self.lam_ref = wax_ref, ba_ref, bx_ref, lam_ref
        self.conv_hist, self.h_carry, self.h_buf, self.yin_buf = conv_hist, h_carry, h_buf, yin_buf

    def project(self):
        lo, cols = self.lo, self.cols
        u_lru = _dot(self.hn, self.win_ref[:, OFF_LRU + lo:OFF_LRU + lo + LRU_GROUP])
        ext = jnp.concatenate([self.conv_hist[:, cols], u_lru], axis=0)
        self.conv_hist[:, cols] = _steps(u_lru, MIX_STEPS - CONV_HIST, CONV_HIST)
        v = self.convb_ref[:, cols] + self.convw_ref[0:1, cols] * _steps(ext, 0, MIX_STEPS)
        for k in range(1, CONV_WIDTH):
            v = v + self.convw_ref[k:k + 1, cols] * _steps(ext, k, MIX_STEPS)
        self.v = v

    def recur(self):
        lo, cols, v = self.lo, self.cols, self.v
        r_parts, i_parts = [], []
        for hd in range(lo // LRU_HEAD_DIM, (lo + LRU_GROUP) // LRU_HEAD_DIM):
            hcols = slice(hd * LRU_HEAD_DIM, (hd + 1) * LRU_HEAD_DIM)
            vh = v[:, hd * LRU_HEAD_DIM - lo:(hd + 1) * LRU_HEAD_DIM - lo].astype(BF16)
            ri = _dot(vh, self.wax_ref[hcols, :])
            r_parts.append(_sigmoid(ri[:, :LRU_HEAD_DIM] + self.ba_ref[:, hcols]))
            i_parts.append(_sigmoid(ri[:, LRU_HEAD_DIM:] + self.bx_ref[:, hcols]))
        r = jnp.concatenate(r_parts, axis=1)
        gate_i = jnp.concatenate(i_parts, axis=1)

        neg_lam = -self.lam_ref[:, cols]
        softplus = jnp.maximum(neg_lam, 0.0) + jnp.log1p(jnp.exp(-jnp.abs(neg_lam)))
        log_a = r * ((-LRU_C) * softplus)
        a = jnp.exp(log_a)
        b = _safe_sqrt(jnp.tanh(log_a) * (-1.0 - a * a)) * (gate_i * v)

        h = self.h_carry[:, cols]
        for t in range(MIX_STEPS):
            h = _steps(a, t, 1) * h + _steps(b, t, 1)
            self.h_buf[t * BATCH:(t + 1) * BATCH, cols] = h
        self.h_carry[:, cols] = h

    def gate(self):
        lo, cols = self.lo, self.cols
        u_gelu = _dot(self.hn, self.win_ref[:, OFF_GELU + lo:OFF_GELU + lo + LRU_GROUP])
        self.yin_buf[:, cols] = (self.h_buf[:, cols] * jax.nn.gelu(u_gelu)).astype(BF16)


class _PoolBranch:
    def __init__(self, hn, tile, win_ref, poolw_ref, poolb_ref, pools_ref, wpu_ref, pool_hist):
        self.hn, self.tile, self.win_ref = hn, tile, win_ref
        self.poolw_ref, self.poolb_ref, self.pools_ref = poolw_ref, poolb_ref, pools_ref
        self.wpu_ref, self.pool_hist = wpu_ref, pool_hist

    def project(self):
        u_pool = _dot(self.hn, self.win_ref[:, OFF_POOL:OFF_POOL + POOL_WIDTH])
        ext = jnp.concatenate([self.pool_hist[...], u_pool], axis=0)
        self.pool_hist[...] = _steps(u_pool, MIX_STEPS - POOL_HIST, POOL_HIST)
        step = self.tile * MIX_STEPS + lax.shift_right_logical(
            lax.broadcasted_iota(jnp.int32, (MIX_TILE, POOL_GROUP_DIM), 0),
            BATCH.bit_length() - 1)
        self.pooled = []
        for grp in range(POOL_GROUPS):
            cols = slice(grp * POOL_GROUP_DIM, (grp + 1) * POOL_GROUP_DIM)
            window = POOL_WINDOWS[grp]
            s = _steps(ext[:, cols], POOL_HIST - (window - 1), MIX_STEPS + window - 1)
            d = 1
            while d < window:
                n = s.shape[0] // BATCH
                s = _steps(s, d, n - d) + _steps(s, 0, n - d)
                d *= 2
            count = jnp.minimum(step + 1, window).astype(F32)
            self.pooled.append((s / count - u_pool[:, cols]).astype(BF16))

    def mix(self):
        mixed = []
        for grp in range(POOL_GROUPS):
            cols = slice(grp * POOL_GROUP_DIM, (grp + 1) * POOL_GROUP_DIM)
            m = _dot(self.pooled[grp], self.poolw_ref[cols, :]) + self.poolb_ref[:, cols]
            mixed.append(m * self.pools_ref[:, cols])
        mixed = jnp.concatenate(mixed, axis=1).astype(BF16)
        return _dot(mixed, self.wpu_ref[...])


def _mixer_kernel(*refs, n_cast):
    (x_ref, g_ref, win_ref, poolw_ref, poolb_ref, pools_ref, wpu_ref, convw_ref, convb_ref,
     wax_ref, ba_ref, bx_ref, lam_ref, wlu_ref, wout_ref) = refs[:15]
    cast_src = refs[15:15 + n_cast]
    o_ref = refs[15 + n_cast]
    cast_dst = refs[16 + n_cast:16 + 2 * n_cast]
    pool_hist, conv_hist, h_carry, h_buf, yin_buf = refs[16 + 2 * n_cast:]
    tile = pl.program_id(0)
    _cast_blocks(cast_src, cast_dst)

    @pl.when(tile == 0)
    def _():
        pool_hist[...] = jnp.zeros_like(pool_hist)
        conv_hist[...] = jnp.zeros_like(conv_hist)
        h_carry[...] = jnp.zeros_like(h_carry)

    hn = _rmsnorm(x_ref[...], g_ref[...]).astype(BF16)
    lru = [_LruGroup(grp, hn, win_ref, convw_ref, convb_ref, wax_ref, ba_ref, bx_ref, lam_ref,
                     conv_hist, h_carry, h_buf, yin_buf)
           for grp in range(LRU_WIDTH // LRU_GROUP)]
    pool = _PoolBranch(hn, tile, win_ref, poolw_ref, poolb_ref, pools_ref, wpu_ref, pool_hist)

    lru[0].project()
    lru[1].project()
    lru[0].recur()
    lru[0].gate()
    lru[2].project()
    lru[1].recur()
    lru[1].gate()
    lru[3].project()
    pool.project()
    lru[2].recur()
    lru[2].gate()
    lru[4].project()
    y_pool = pool.mix()
    lru[3].recur()
    lru[3].gate()
    z = _sigmoid(_dot(hn, win_ref[:, OFF_GPOOL:OFF_GPOOL + D_MODEL])) * y_pool
    lru[4].recur()
    lru[4].gate()
    g_lru = _sigmoid(_dot(hn, win_ref[:, OFF_GLRU:OFF_GLRU + D_MODEL]))
    z = z + g_lru * _dot(yin_buf[...], wlu_ref[...])

    o_ref[...] = x_ref[...] + _dot(z.astype(BF16), wout_ref[...])


def _mixer(x, layer, g, w_in, pool_w, pool_b, pool_scale, w_pool_up, conv_w, conv_b,
           w_ax, b_a, b_x, lam, w_lru_up, w_out, cast_next, cast_layer):
    n_steps = SEQ // MIX_STEPS
    tile_spec = pl.BlockSpec((MIX_TILE, D_MODEL), lambda i: (i, 0))
    cast_in, cast_out, cast_shapes = _cast_specs(cast_next, cast_layer, n_steps)
    outs = pl.pallas_call(
        functools.partial(_mixer_kernel, n_cast=len(cast_next)),
        grid=(n_steps,),
        in_specs=[
            tile_spec, _resident(g, layer), _resident(w_in), _resident(pool_w),
            _resident(pool_b, layer), _resident(pool_scale, layer), _resident(w_pool_up),
            _resident(conv_w, layer), _resident(conv_b, layer), _resident(w_ax),
            _resident(b_a, layer), _resident(b_x, layer), _resident(lam, layer),
            _resident(w_lru_up), _resident(w_out),
        ] + cast_in,
        out_specs=[tile_spec] + cast_out,
        out_shape=[jax.ShapeDtypeStruct(x.shape, x.dtype)] + cast_shapes,
        scratch_shapes=[
            pltpu.VMEM((POOL_HIST * BATCH, POOL_WIDTH), F32),
            pltpu.VMEM((CONV_HIST * BATCH, LRU_WIDTH), F32),
            pltpu.VMEM((BATCH, LRU_WIDTH), F32),
            pltpu.VMEM((MIX_TILE, LRU_WIDTH), F32),
            pltpu.VMEM((MIX_TILE, LRU_WIDTH), BF16),
        ],
        compiler_params=pltpu.CompilerParams(
            dimension_semantics=("arbitrary",), vmem_limit_bytes=V7X_VMEM_LIMIT),
        name="mixer",
    )(x, g, w_in, pool_w, pool_b, pool_scale, w_pool_up, conv_w, conv_b, w_ax, b_a, b_x, lam,
      w_lru_up, w_out, *cast_next)
    return outs[0], tuple(outs[1:])


def kernel(x, norm_ffn1, ffn1_w_up, ffn1_w_down, norm_mix, w_in, pool_w, pool_b, pool_scale,
           w_pool_up, conv_w, conv_b, lru_w_a, lru_b_a, lru_w_x, lru_b_x, lru_lambda, w_lru_up,
           w_out, norm_ffn2, ffn2_w_up, ffn2_w_down, final_norm):
    assert x.shape == (BATCH, SEQ, D_MODEL) and norm_ffn1.shape[0] == DEPTH

    def row(p):
        return p.reshape(DEPTH, 1, -1)

    def rows(p):
        return p.reshape(DEPTH, -1, p.shape[-1])

    g_final = final_norm.reshape(1, D_MODEL)
    ffn1_f32 = (ffn1_w_up, ffn1_w_down)
    ffn2_f32 = (ffn2_w_up, ffn2_w_down)
    mix_f32 = (w_in, rows(pool_w), w_pool_up, rows(jnp.concatenate([lru_w_a, lru_w_x], axis=-1)),
               w_lru_up, w_out)

    ffn_w = (ffn1_w_up[0].astype(BF16), ffn1_w_down[0].astype(BF16))
    h = x
    for layer in range(DEPTH):
        last = layer == DEPTH - 1
        h, mix_w = _ffn(h, row(norm_ffn1), layer, *ffn_w, g_final, mix_f32, layer,
                        batch_major_in=layer == 0)
        w_in_b, pool_w_b, w_pool_up_b, w_ax_b, w_lru_up_b, w_out_b = mix_w
        h, ffn_w = _mixer(h, layer, row(norm_mix), w_in_b, pool_w_b, row(pool_b), row(pool_scale),
                          w_pool_up_b, conv_w, row(conv_b), w_ax_b, row(lru_b_a), row(lru_b_x),
                          row(lru_lambda), w_lru_up_b, w_out_b, ffn2_f32, layer)
        h, ffn_w = _ffn(h, row(norm_ffn2), layer, *ffn_w, g_final,
                        () if last else ffn1_f32, layer + 1,
                        final_norm=last, batch_major_out=last)
    return h
```

```python
import functools
import math

import jax
import jax.numpy as jnp
from jax import lax
from jax.experimental import pallas as pl
from jax.experimental.pallas import tpu as pltpu

D_MODEL = 1024
BATCH = 8
SEQ = 2048
DEPTH = 4
POOL_GROUPS = 4
POOL_GROUP_DIM = 128
POOL_WIDTH = POOL_GROUPS * POOL_GROUP_DIM
POOL_WINDOWS = (2, 4, 8, 16)
LRU_HEADS = 10
LRU_HEAD_DIM = 128
LRU_WIDTH = LRU_HEADS * LRU_HEAD_DIM
CONV_WIDTH = 4
LRU_C = 8.0
D_FF = 2816
EPS = 1e-6

OFF_POOL = 0
OFF_LRU = POOL_WIDTH
OFF_GELU = OFF_LRU + LRU_WIDTH
OFF_GPOOL = OFF_GELU + LRU_WIDTH
OFF_GLRU = OFF_GPOOL + D_MODEL

V7X_SUBLANES = 8
V7X_VMEM_LIMIT = 60000 * 1024

FFN_TILE = 1024
FFN_CHUNK = 256
MIX_STEPS = 128
MIX_TILE = MIX_STEPS * BATCH
POOL_HIST = max(POOL_WINDOWS) - 1
CONV_HIST = CONV_WIDTH - 1
LRU_GROUP = 2 * LRU_HEAD_DIM

F32 = jnp.float32
BF16 = jnp.bfloat16

assert BATCH == V7X_SUBLANES


def _rmsnorm(x, g):
    var = jnp.mean(x * x, axis=-1, keepdims=True)
    return x * lax.rsqrt(var + EPS) * g


def _tanh_half(x):
    return jnp.tanh(0.5 * x)


def _sigmoid(x):
    return 0.5 * _tanh_half(x) + 0.5


GELU_C1 = math.sqrt(2.0 / math.pi)
GELU_C2 = GELU_C1 * 0.044715


def _times_gelu(h, u):
    t = jnp.tanh(u * (GELU_C1 + GELU_C2 * (u * u)))
    return (0.5 * (h * u)) * (1.0 + t)


_dot = functools.partial(jnp.dot, preferred_element_type=F32)


def _resident(arr, layer=None):
    if layer is None:
        return pl.BlockSpec(arr.shape, lambda i: (0,) * arr.ndim, pipeline_mode=pl.Buffered(1))
    zeros = (0,) * (arr.ndim - 1)
    return pl.BlockSpec((None,) + tuple(arr.shape[1:]), lambda i: (layer,) + zeros,
                        pipeline_mode=pl.Buffered(1))


def _cast_specs(stacked, layer, n_steps):
    in_specs, out_specs, out_shapes = [], [], []
    for w in stacked:
        _, rows, cols = w.shape
        assert rows % (n_steps * 16) == 0, (w.shape, n_steps)
        blk = rows // n_steps
        in_specs.append(pl.BlockSpec((None, blk, cols), lambda i, layer=layer: (layer, i, 0)))
        out_specs.append(pl.BlockSpec((blk, cols), lambda i: (i, 0)))
        out_shapes.append(jax.ShapeDtypeStruct((rows, cols), BF16))
    return in_specs, out_specs, out_shapes


def _cast_blocks(src_refs, dst_refs):
    for s_ref, d_ref in zip(src_refs, dst_refs):
        d_ref[...] = s_ref[...].astype(BF16)


def _ffn_kernel(*refs, n_cast, final_norm, batch_major_in, batch_major_out):
    x_ref, g_ref, wup_ref, wdn_ref, gfin_ref = refs[:5]
    cast_src = refs[5:5 + n_cast]
    o_ref = refs[5 + n_cast]
    cast_dst = refs[6 + n_cast:6 + 2 * n_cast]
    hn_ref, gated_ref, x_tm = refs[6 + 2 * n_cast:]

    _cast_blocks(cast_src, cast_dst)
    if batch_major_in:
        x_tm[...] = jnp.transpose(x_ref[...], (1, 0, 2)).reshape(FFN_TILE, D_MODEL)
        x_ref = x_tm
    hn_ref[...] = _rmsnorm(x_ref[...], g_ref[...]).astype(BF16)
    for j in range(D_FF // FFN_CHUNK):
        lo = j * FFN_CHUNK
        hn = hn_ref[...]
        a = _dot(hn, wup_ref[:, lo:lo + FFN_CHUNK])
        b = _dot(hn, wup_ref[:, D_FF + lo:D_FF + lo + FFN_CHUNK])
        half_a = 0.5 * a
        gated = half_a * (1.0 + jnp.tanh(half_a)) * b
        gated_ref[:, lo:lo + FFN_CHUNK] = gated.astype(BF16)
    y = x_ref[...] + 0.5 * _dot(gated_ref[...], wdn_ref[...])
    if final_norm:
        y = _rmsnorm(y, gfin_ref[...])
    if batch_major_out:
        y = jnp.transpose(y.reshape(FFN_TILE // BATCH, BATCH, D_MODEL), (1, 0, 2))
    o_ref[...] = y


def _ffn(x, g, layer, w_up, w_dn, g_final, cast_next, cast_layer, *, final_norm=False,
         batch_major_in=False, batch_major_out=False):
    n_steps = SEQ * BATCH // FFN_TILE
    steps_per_tile = FFN_TILE // BATCH
    bm_spec = pl.BlockSpec((BATCH, steps_per_tile, D_MODEL), lambda i: (0, i, 0))
    tm_spec = pl.BlockSpec((FFN_TILE, D_MODEL), lambda i: (i, 0))
    cast_in, cast_out, cast_shapes = _cast_specs(cast_next, cast_layer, n_steps)
    out_shape = (BATCH, SEQ, D_MODEL) if batch_major_out else (SEQ * BATCH, D_MODEL)
    outs = pl.pallas_call(
        functools.partial(_ffn_kernel, n_cast=len(cast_next), final_norm=final_norm,
                          batch_major_in=batch_major_in, batch_major_out=batch_major_out),
        grid=(n_steps,),
        in_specs=[
            bm_spec if batch_major_in else tm_spec,
            _resident(g, layer),
            _resident(w_up),
            _resident(w_dn),
            _resident(g_final),
        ] + cast_in,
        out_specs=[bm_spec if batch_major_out else tm_spec] + cast_out,
        out_shape=[jax.ShapeDtypeStruct(out_shape, F32)] + cast_shapes,
        scratch_shapes=[
            pltpu.VMEM((FFN_TILE, D_MODEL), BF16),
            pltpu.VMEM((FFN_TILE, D_FF), BF16),
            pltpu.VMEM((FFN_TILE, D_MODEL), F32),
        ],
        compiler_params=pltpu.CompilerParams(
            dimension_semantics=("arbitrary",), vmem_limit_bytes=V7X_VMEM_LIMIT),
        name="ffn",
    )(x, g, w_up, w_dn, g_final, *cast_next)
    return outs[0], tuple(outs[1:])


def _steps(x, first, count):
    return x[first * BATCH:(first + count) * BATCH, :]


def _safe_sqrt(x):
    return x * lax.rsqrt(jnp.maximum(x, jnp.finfo(jnp.float32).tiny))


class _LruGroup:
    def __init__(self, grp, hn, win_ref, convw_ref, convb_ref, wax_ref, ba_ref, bx_ref, lam_ref,
                 conv_hist, h_carry, h_buf, yin_buf):
        self.lo = grp * LRU_GROUP
        self.cols = slice(self.lo, self.lo + LRU_GROUP)
        self.hn = hn
        self.win_ref, self.convw_ref, self.convb_ref = win_ref, convw_ref, convb_ref
        self.wax_ref, self.ba_ref, self.bx_ref, self.lam_ref = wax_ref, ba_ref, bx_ref, lam_ref
        self.conv_hist, self.h_carry, self.h_buf, self.yin_buf = conv_hist, h_carry, h_buf, yin_buf

    def project(self):
        lo, cols = self.lo, self.cols
        u_lru = _dot(self.hn, self.win_ref[:, OFF_LRU + lo:OFF_LRU + lo + LRU_GROUP])
        ext = jnp.concatenate([self.conv_hist[:, cols], u_lru], axis=0)
        self.conv_hist[:, cols] = _steps(u_lru, MIX_STEPS - CONV_HIST, CONV_HIST)
        v = self.convb_ref[:, cols] + self.convw_ref[0:1, cols] * _steps(ext, 0, MIX_STEPS)
        for k in range(1, CONV_WIDTH):
            v = v + self.convw_ref[k:k + 1, cols] * _steps(ext, k, MIX_STEPS)
        self.v = v

    def recur(self):
        lo, cols, v = self.lo, self.cols, self.v
        r_parts, i_parts = [], []
        for hd in range(lo // LRU_HEAD_DIM, (lo + LRU_GROUP) // LRU_HEAD_DIM):
            hcols = slice(hd * LRU_HEAD_DIM, (hd + 1) * LRU_HEAD_DIM)
            vh = v[:, hd * LRU_HEAD_DIM - lo:(hd + 1) * LRU_HEAD_DIM - lo].astype(BF16)
            ri = _dot(vh, self.wax_ref[hcols, :])
            r_parts.append(_tanh_half(ri[:, :LRU_HEAD_DIM] + self.ba_ref[:, hcols]))
            i_parts.append(_sigmoid(ri[:, LRU_HEAD_DIM:] + self.bx_ref[:, hcols]))
        r_tanh = jnp.concatenate(r_parts, axis=1)
        gate_i = jnp.concatenate(i_parts, axis=1)

        neg_lam = -self.lam_ref[:, cols]
        softplus = jnp.maximum(neg_lam, 0.0) + jnp.log1p(jnp.exp(-jnp.abs(neg_lam)))
        half_c = (-0.5 * LRU_C) * softplus
        log_a = r_tanh * half_c + half_c
        a = jnp.exp(log_a)
        b = _safe_sqrt(jnp.tanh(log_a) * (-1.0 - a * a)) * (gate_i * v)

        h = self.h_carry[:, cols]
        for t in range(MIX_STEPS):
            h = _steps(a, t, 1) * h + _steps(b, t, 1)
            self.h_buf[t * BATCH:(t + 1) * BATCH, cols] = h
        self.h_carry[:, cols] = h

    def gate(self):
        lo, cols = self.lo, self.cols
        u_gelu = _dot(self.hn, self.win_ref[:, OFF_GELU + lo:OFF_GELU + lo + LRU_GROUP])
        self.yin_buf[:, cols] = _times_gelu(self.h_buf[:, cols], u_gelu).astype(BF16)


class _PoolBranch:
    def __init__(self, hn, tile, win_ref, poolw_ref, poolb_ref, pools_ref, wpu_ref, pool_hist):
        self.hn, self.tile, self.win_ref = hn, tile, win_ref
        self.poolw_ref, self.poolb_ref, self.pools_ref = poolw_ref, poolb_ref, pools_ref
        self.wpu_ref, self.pool_hist = wpu_ref, pool_hist

    def project(self):
        u_pool = _dot(self.hn, self.win_ref[:, OFF_POOL:OFF_POOL + POOL_WIDTH])
        ext = jnp.concatenate([self.pool_hist[...], u_pool], axis=0)
        self.pool_hist[...] = _steps(u_pool, MIX_STEPS - POOL_HIST, POOL_HIST)
        step = self.tile * MIX_STEPS + lax.shift_right_logical(
            lax.broadcasted_iota(jnp.int32, (MIX_TILE, POOL_GROUP_DIM), 0),
            BATCH.bit_length() - 1)
        self.pooled = []
        for grp in range(POOL_GROUPS):
            cols = slice(grp * POOL_GROUP_DIM, (grp + 1) * POOL_GROUP_DIM)
            window = POOL_WINDOWS[grp]
            s = _steps(ext[:, cols], POOL_HIST - (window - 1), MIX_STEPS + window - 1)
            d = 1
            while d < window:
                n = s.shape[0] // BATCH
                s = _steps(s, d, n - d) + _steps(s, 0, n - d)
                d *= 2
            count = jnp.minimum(step + 1, window).astype(F32)
            self.pooled.append((s / count - u_pool[:, cols]).astype(BF16))

    def mix(self):
        mixed = []
        for grp in range(POOL_GROUPS):
            cols = slice(grp * POOL_GROUP_DIM, (grp + 1) * POOL_GROUP_DIM)
            m = _dot(self.pooled[grp], self.poolw_ref[cols, :]) + self.poolb_ref[:, cols]
            mixed.append(m * self.pools_ref[:, cols])
        self.mixed = jnp.concatenate(mixed, axis=1).astype(BF16)

    def up(self):
        return _dot(self.mixed, self.wpu_ref[...])


def _mixer_kernel(*refs, n_cast):
    (x_ref, g_ref, win_ref, poolw_ref, poolb_ref, pools_ref, wpu_ref, convw_ref, convb_ref,
     wax_ref, ba_ref, bx_ref, lam_ref, wlu_ref, wout_ref) = refs[:15]
    cast_src = refs[15:15 + n_cast]
    o_ref = refs[15 + n_cast]
    cast_dst = refs[16 + n_cast:16 + 2 * n_cast]
    pool_hist, conv_hist, h_carry, h_buf, yin_buf = refs[16 + 2 * n_cast:]
    tile = pl.program_id(0)
    _cast_blocks(cast_src, cast_dst)

    @pl.when(tile == 0)
    def _():
        pool_hist[...] = jnp.zeros_like(pool_hist)
        conv_hist[...] = jnp.zeros_like(conv_hist)
        h_carry[...] = jnp.zeros_like(h_carry)

    hn = _rmsnorm(x_ref[...], g_ref[...]).astype(BF16)
    lru = [_LruGroup(grp, hn, win_ref, convw_ref, convb_ref, wax_ref, ba_ref, bx_ref, lam_ref,
                     conv_hist, h_carry, h_buf, yin_buf)
           for grp in range(LRU_WIDTH // LRU_GROUP)]
    pool = _PoolBranch(hn, tile, win_ref, poolw_ref, poolb_ref, pools_ref, wpu_ref, pool_hist)

    half = D_MODEL // 2

    def gate_tanh(off, part):
        lo = off + part * half
        return _tanh_half(_dot(hn, win_ref[:, lo:lo + half]))

    lru[0].project()
    lru[1].project()
    t_pool_0 = gate_tanh(OFF_GPOOL, 0)
    lru[0].recur()
    lru[0].gate()
    lru[2].project()
    t_pool_1 = gate_tanh(OFF_GPOOL, 1)
    lru[1].recur()
    lru[1].gate()
    lru[3].project()
    pool.project()
    lru[2].recur()
    lru[2].gate()
    lru[4].project()
    t_lru_0 = gate_tanh(OFF_GLRU, 0)
    lru[3].recur()
    lru[3].gate()
    pool.mix()
    lru[4].recur()
    lru[4].gate()
    t_lru_1 = gate_tanh(OFF_GLRU, 1)
    y_pool = pool.up()
    t_pool = jnp.concatenate([t_pool_0, t_pool_1], axis=1)
    t_lru = jnp.concatenate([t_lru_0, t_lru_1], axis=1)
    y_lru = _dot(yin_buf[...], wlu_ref[...])
    z = 0.5 * ((1.0 + t_pool) * y_pool + (1.0 + t_lru) * y_lru)

    o_ref[...] = x_ref[...] + _dot(z.astype(BF16), wout_ref[...])


def _mixer(x, layer, g, w_in, pool_w, pool_b, pool_scale, w_pool_up, conv_w, conv_b,
           w_ax, b_a, b_x, lam, w_lru_up, w_out, cast_next, cast_layer):
    n_steps = SEQ // MIX_STEPS
    tile_spec = pl.BlockSpec((MIX_TILE, D_MODEL), lambda i: (i, 0))
    cast_in, cast_out, cast_shapes = _cast_specs(cast_next, cast_layer, n_steps)
    outs = pl.pallas_call(
        functools.partial(_mixer_kernel, n_cast=len(cast_next)),
        grid=(n_steps,),
        in_specs=[
            tile_spec, _resident(g, layer), _resident(w_in), _resident(pool_w),
            _resident(pool_b, layer), _resident(pool_scale, layer), _resident(w_pool_up),
            _resident(conv_w, layer), _resident(conv_b, layer), _resident(w_ax),
            _resident(b_a, layer), _resident(b_x, layer), _resident(lam, layer),
            _resident(w_lru_up), _resident(w_out),
        ] + cast_in,
        out_specs=[tile_spec] + cast_out,
        out_shape=[jax.ShapeDtypeStruct(x.shape, x.dtype)] + cast_shapes,
        scratch_shapes=[
            pltpu.VMEM((POOL_HIST * BATCH, POOL_WIDTH), F32),
            pltpu.VMEM((CONV_HIST * BATCH, LRU_WIDTH), F32),
            pltpu.VMEM((BATCH, LRU_WIDTH), F32),
            pltpu.VMEM((MIX_TILE, LRU_WIDTH), F32),
            pltpu.VMEM((MIX_TILE, LRU_WIDTH), BF16),
        ],
        compiler_params=pltpu.CompilerParams(
            dimension_semantics=("arbitrary",), vmem_limit_bytes=V7X_VMEM_LIMIT),
        name="mixer",
    )(x, g, w_in, pool_w, pool_b, pool_scale, w_pool_up, conv_w, conv_b, w_ax, b_a, b_x, lam,
      w_lru_up, w_out, *cast_next)
    return outs[0], tuple(outs[1:])


def kernel(x, norm_ffn1, ffn1_w_up, ffn1_w_down, norm_mix, w_in, pool_w, pool_b, pool_scale,
           w_pool_up, conv_w, conv_b, lru_w_a, lru_b_a, lru_w_x, lru_b_x, lru_lambda, w_lru_up,
           w_out, norm_ffn2, ffn2_w_up, ffn2_w_down, final_norm):
    assert x.shape == (BATCH, SEQ, D_MODEL) and norm_ffn1.shape[0] == DEPTH

    def row(p):
        return p.reshape(DEPTH, 1, -1)

    def rows(p):
        return p.reshape(DEPTH, -1, p.shape[-1])

    g_final = final_norm.reshape(1, D_MODEL)
    ffn1_f32 = (ffn1_w_up, ffn1_w_down)
    ffn2_f32 = (ffn2_w_up, ffn2_w_down)
    mix_f32 = (w_in, rows(pool_w), w_pool_up, rows(jnp.concatenate([lru_w_a, lru_w_x], axis=-1)),
               w_lru_up, w_out)

    ffn_w = (ffn1_w_up[0].astype(BF16), ffn1_w_down[0].astype(BF16))
    h = x
    for layer in range(DEPTH):
        last = layer == DEPTH - 1
        h, mix_w = _ffn(h, row(norm_ffn1), layer, *ffn_w, g_final, mix_f32, layer,
                        batch_major_in=layer == 0)
        w_in_b, pool_w_b, w_pool_up_b, w_ax_b, w_lru_up_b, w_out_b = mix_w
        h, ffn_w = _mixer(h, layer, row(norm_mix), w_in_b, pool_w_b, row(pool_b), row(pool_scale),
                          w_pool_up_b, conv_w, row(conv_b), w_ax_b, row(lru_b_a), row(lru_b_x),
                          row(lru_lambda), w_lru_up_b, w_out_b, ffn2_f32, layer)
        h, ffn_w = _ffn(h, row(norm_ffn2), layer, *ffn_w, g_final,
                        () if last else ffn1_f32, layer + 1,
                        final_norm=last, batch_major_out=last)
    return h
```
